```python
import math
import jax, jax.numpy as jnp
from jax import lax
import numpy as np

D_MODEL = 1024
BATCH = 32
SEQ = 256
DEPTH = 4
DEC_BATCH = 8
DEC_SEQ = 1024
PAST_LEN = 256

GRID_W = 64
CHUNK = 128
EPS = 1e-6
A_WIDTH = 256
A_GROUPS = 4
B_WIDTH = 256
CONV_W = 3
FILTER_EMB = 33
FILTER_HIDDEN = 64
FILTER_FAST_DECAY = 0.3
FILTER_SLOW_DECAY = 1.5
FILTER_TARGET = 1e-2
MLA_HEADS = 8
Q_RANK = 256
KV_RANK = 128
NOPE_DIM = 64
ROPE_DIM = 32
V_DIM = 64
ROPE_BASE = 10000.0
GATE_COLS = 3 * D_MODEL
A_COLS = 2 * A_WIDTH
B_COLS = 3 * B_WIDTH
C_COLS = Q_RANK + KV_RANK + ROPE_DIM
IN_COLS = GATE_COLS + A_COLS + B_COLS + C_COLS
PEER_HEADS = 8
N_KEYS = 128
N_EXPERTS = N_KEYS * N_KEYS
PEER_KEY_DIM = 256
PEER_TOPK = 16
PEER_BLOCK = 128

kernel_name = 'hybrid_gmlp_hyena_mla_peer_diffusion_step'


def rms_norm(x, g):
    xf = x.astype(jnp.float32)
    y = xf * lax.rsqrt(jnp.mean(xf * xf, axis=-1, keepdims=True) + EPS)
    return (y * g.astype(jnp.float32)).astype(x.dtype)


def axial_rope_tables(n_tokens, dtype):
    rows = n_tokens // GRID_W
    row = jnp.broadcast_to(jnp.arange(rows)[:, None], (rows, GRID_W)).reshape(-1)
    col = jnp.broadcast_to(jnp.arange(GRID_W)[None, :], (rows, GRID_W)).reshape(-1)
    quarter = ROPE_DIM // 4
    inv_freq = ROPE_BASE ** (-jnp.arange(quarter, dtype=jnp.float32) / quarter)
    ang = jnp.stack([row, col], axis=-1).astype(jnp.float32)[:, :, None] * inv_freq
    return jnp.cos(ang).astype(dtype), jnp.sin(ang).astype(dtype)


def apply_axial_rope(x, cos, sin):
    xr = x.reshape(x.shape[:-1] + (2, 2, ROPE_DIM // 4))
    x1, x2 = xr[..., 0, :], xr[..., 1, :]
    out = jnp.stack([x1 * cos - x2 * sin, x2 * cos + x1 * sin], axis=-2)
    return out.reshape(x.shape)


def chunk_gating_mlp(z, w_s, b_s):
    bsz, L, _ = z.shape
    z = jax.nn.gelu(z)
    u, v = jnp.split(z, 2, axis=-1)
    v = v.reshape(bsz, L // CHUNK, CHUNK, A_GROUPS, A_WIDTH // A_GROUPS)
    mixed = jnp.einsum('gpq,bnqgc->bnpgc', w_s, v) + b_s.T[:, :, None]
    return u * mixed.reshape(bsz, L, A_WIDTH)


def short_conv(z, w, b):
    L = z.shape[1]
    pad = CONV_W // 2
    zp = jnp.pad(z, ((0, 0), (pad, CONV_W - 1 - pad), (0, 0)))
    out = b
    for k in range(CONV_W):
        out = out + zp[:, k:k + L] * w[k]
    return out


def hyena_filter(L, w1, b1, w2, b2, w3):
    f32 = jnp.float32
    t = jnp.linspace(0.0, 1.0, L, dtype=f32)[:, None]
    bands = (FILTER_EMB - 1) // 2
    f = jnp.linspace(1e-4, bands - 1, bands, dtype=f32)
    w = 2.0 * math.pi * jnp.arange(L, dtype=f32)[:, None] / L
    z = jnp.concatenate([t, jnp.cos(f * w), -jnp.sin(f * w)], axis=-1)
    h = jnp.sin(z @ w1.astype(f32) + b1.astype(f32))
    h = jnp.sin(h @ w2.astype(f32) + b2.astype(f32))
    h = (h @ w3.astype(f32)).reshape(L, 2, B_WIDTH)
    deltas = jnp.abs(jnp.linspace(math.log(FILTER_TARGET) / FILTER_FAST_DECAY,
                                  math.log(FILTER_TARGET) / FILTER_SLOW_DECAY, B_WIDTH, dtype=f32))
    h = h * jnp.exp(-t * deltas)[:, None, :]
    fwd, bwd = h[:, 0], h[:, 1]
    return jnp.concatenate([fwd, jnp.zeros((1, B_WIDTH), f32), bwd[:0:-1]], axis=0)


def long_conv(u, h_circ, skip):
    L = u.shape[1]
    uf = u.astype(jnp.float32)
    U = jnp.fft.rfft(uf, n=2 * L, axis=1)
    Hf = jnp.fft.rfft(h_circ, n=2 * L, axis=0)
    y = jnp.fft.irfft(U * Hf[None], n=2 * L, axis=1)[:, :L]
    return (y + uf * skip.astype(jnp.float32)).astype(u.dtype)


def hyena_branch(z, conv_w, conv_b, w1, b1, w2, b2, w3, skip):
    z = short_conv(z, conv_w, conv_b)
    x0, x1, v = jnp.split(z, 3, axis=-1)
    h_circ = hyena_filter(z.shape[1], w1, b1, w2, b2, w3)
    return x0 * long_conv(x1 * v, h_circ, skip)


def latent_attention(q_lat, q_pe, ckv, kpe):
    bsz, Lq, H, _ = q_lat.shape
    nb = Lq // CHUNK
    scale = (NOPE_DIM + ROPE_DIM) ** -0.5

    def to_blocks(a):
        return jnp.moveaxis(a.reshape((bsz, nb, CHUNK) + a.shape[2:]), 1, 0)

    def block(args):
        ql, qp = args
        s = (jnp.einsum('bqhc,bkc->bhqk', ql, ckv) + jnp.einsum('bqhr,bkr->bhqk', qp, kpe)).astype(jnp.float32) * scale
        p = jax.nn.softmax(s, axis=-1).astype(ckv.dtype)
        return jnp.einsum('bhqk,bkc->bqhc', p, ckv)

    o = lax.map(block, (to_blocks(q_lat), to_blocks(q_pe)))
    return jnp.moveaxis(o, 0, 1).reshape(bsz, Lq, H, KV_RANK)


def peer_ffn(h, w_q, sub_keys, u_tab, v_tab):
    bsz, L, D = h.shape
    tok = h.reshape(-1, D)
    T = tok.shape[0]
    q = (tok @ w_q).reshape(T, PEER_HEADS, 2, PEER_KEY_DIM // 2)
    s = jnp.einsum('thsk,snk->thsn', q, sub_keys).astype(jnp.float32)
    sv, si = lax.top_k(s, PEER_TOPK)
    cand_s = (sv[..., 0, :, None] + sv[..., 1, None, :]).reshape(T, PEER_HEADS, PEER_TOPK * PEER_TOPK)
    cand_i = (si[..., 0, :, None] * N_KEYS + si[..., 1, None, :]).reshape(T, PEER_HEADS, PEER_TOPK * PEER_TOPK)
    top_s, top_pos = lax.top_k(cand_s, PEER_TOPK)
    idx = jnp.take_along_axis(cand_i, top_pos, axis=-1)
    g = jax.nn.softmax(top_s, axis=-1).astype(h.dtype)
    nb = T // PEER_BLOCK

    def block(args):
        tb, ib, gb = args
        a = jax.nn.gelu(jnp.einsum('tkd,td->tk', jnp.take(u_tab, ib, axis=0), tb))
        return jnp.einsum('tk,tkd->td', gb * a, jnp.take(v_tab, ib, axis=0))

    out = lax.map(block, (tok.reshape(nb, PEER_BLOCK, D),
                          idx.reshape(nb, PEER_BLOCK, PEER_HEADS * PEER_TOPK),
                          g.reshape(nb, PEER_BLOCK, PEER_HEADS * PEER_TOPK)))
    return out.reshape(bsz, L, D)


def trunk_layer(x, cond, lp, ctx_kv, rope):
    bsz, L, _ = x.shape
    mod = jax.nn.silu(cond) @ lp['w_ada'] + lp['b_ada']
    sh1, sc1, g1, sh2, sc2, g2 = jnp.split(mod[:, None, :], 6, axis=-1)
    h = rms_norm(x, lp['norm1']) * (1.0 + sc1) + sh1
    proj = h @ lp['w_in']
    c0 = GATE_COLS
    c1 = c0 + A_COLS
    c2 = c1 + B_COLS
    c3 = c2 + Q_RANK
    c4 = c3 + KV_RANK
    gates, za, zb, zq, zkv, kpe = jnp.split(proj, [c0, c1, c2, c3, c4], axis=-1)
    ya = chunk_gating_mlp(za, lp['a_ws'], lp['a_bs'])
    yb = hyena_branch(zb, lp['b_conv_w'], lp['b_conv_b'], lp['b_filt_w1'], lp['b_filt_b1'],
                      lp['b_filt_w2'], lp['b_filt_b2'], lp['b_filt_w3'], lp['b_skip'])
    q = (rms_norm(zq, lp['q_norm']) @ lp['w_uq']).reshape(bsz, L, MLA_HEADS, NOPE_DIM + ROPE_DIM)
    q_nope, q_pe = q[..., :NOPE_DIM], q[..., NOPE_DIM:]
    ckv = rms_norm(zkv, lp['kv_norm'])
    own_ckv, own_kpe = ckv, kpe
    if rope is not None:
        cos, sin = rope
        q_pe = apply_axial_rope(q_pe, cos[:, None], sin[:, None])
        kpe = apply_axial_rope(kpe, cos, sin)
    if ctx_kv is not None:
        ckv = jnp.concatenate([ckv, ctx_kv[0]], axis=1)
        kpe = jnp.concatenate([kpe, ctx_kv[1]], axis=1)
    q_lat = jnp.einsum('blhn,chn->blhc', q_nope, lp['w_uk'])
    o_lat = latent_attention(q_lat, q_pe, ckv, kpe)
    yc = jnp.einsum('blhc,chv->blhv', o_lat, lp['w_uv']).reshape(bsz, L, MLA_HEADS * V_DIM)
    ga, gb, gc = jnp.split(jax.nn.sigmoid(gates), 3, axis=-1)
    merged = ga * (ya @ lp['w_branch_a']) + gb * (yb @ lp['w_branch_b']) + gc * (yc @ lp['w_branch_c'])
    x = x + g1 * (merged @ lp['w_out'])
    h2 = rms_norm(x, lp['norm2']) * (1.0 + sc2) + sh2
    x = x + g2 * peer_ffn(h2, lp['peer_wq'], lp['peer_sub_keys'], lp['peer_u'], lp['peer_v'])
    return x, own_ckv, own_kpe


def setup_inputs(seed: int = 0) -> dict:
    key = jax.random.key(seed)
    ks = iter(jax.random.split(key, 48))
    D = D_MODEL

    def nrm(shape, scale):
        return jax.random.normal(next(ks), shape, jnp.float32) * scale

    return {
        'x_prompt': nrm((BATCH, SEQ, D), 1.0),
        'x_sample': nrm((DEC_BATCH, DEC_SEQ, D), 1.0),
        'cache_ckv': nrm((DEC_BATCH, DEPTH, PAST_LEN, KV_RANK), 1.0),
        'cache_kpe': nrm((DEC_BATCH, DEPTH, PAST_LEN, ROPE_DIM), 1.0),
        'c': nrm((DEC_BATCH, D), 1.0),
        'c_ctx': nrm((D,), 1.0),
        'w_ada': nrm((DEPTH, D, 6 * D), 0.5 * D ** -0.5),
        'b_ada': nrm((DEPTH, 6 * D), 0.02),
        'norm1': 1.0 + nrm((DEPTH, D), 0.02),
        'norm2': 1.0 + nrm((DEPTH, D), 0.02),
        'w_in': nrm((DEPTH, D, IN_COLS), D ** -0.5),
        'a_ws': nrm((DEPTH, A_GROUPS, CHUNK, CHUNK), CHUNK ** -0.5),
        'a_bs': 1.0 + nrm((DEPTH, A_GROUPS, CHUNK), 0.02),
        'b_conv_w': nrm((DEPTH, CONV_W, B_COLS), CONV_W ** -0.5),
        'b_conv_b': nrm((DEPTH, B_COLS), 0.02),
        'b_filt_w1': nrm((DEPTH, FILTER_EMB, FILTER_HIDDEN), FILTER_EMB ** -0.5),
        'b_filt_b1': nrm((DEPTH, FILTER_HIDDEN), 0.02),
        'b_filt_w2': nrm((DEPTH, FILTER_HIDDEN, FILTER_HIDDEN), FILTER_HIDDEN ** -0.5),
        'b_filt_b2': nrm((DEPTH, FILTER_HIDDEN), 0.02),
        'b_filt_w3': nrm((DEPTH, FILTER_HIDDEN, 2 * B_WIDTH), 0.05 * FILTER_HIDDEN ** -0.5),
        'b_skip': nrm((DEPTH, B_WIDTH), 0.1),
        'q_norm': 1.0 + nrm((DEPTH, Q_RANK), 0.02),
        'kv_norm': 1.0 + nrm((DEPTH, KV_RANK), 0.02),
        'w_uq': nrm((DEPTH, Q_RANK, MLA_HEADS * (NOPE_DIM + ROPE_DIM)), Q_RANK ** -0.5),
        'w_uk': nrm((DEPTH, KV_RANK, MLA_HEADS, NOPE_DIM), KV_RANK ** -0.5),
        'w_uv': nrm((DEPTH, KV_RANK, MLA_HEADS, V_DIM), KV_RANK ** -0.5),
        'w_branch_a': nrm((DEPTH, A_WIDTH, D), A_WIDTH ** -0.5),
        'w_branch_b': nrm((DEPTH, B_WIDTH, D), B_WIDTH ** -0.5),
        'w_branch_c': nrm((DEPTH, MLA_HEADS * V_DIM, D), (MLA_HEADS * V_DIM) ** -0.5),
        'w_out': nrm((DEPTH, D, D), D ** -0.5),
        'peer_wq': nrm((DEPTH, D, PEER_HEADS * PEER_KEY_DIM), D ** -0.5),
        'peer_sub_keys': nrm((DEPTH, 2, N_KEYS, PEER_KEY_DIM // 2), (PEER_KEY_DIM // 2) ** -0.5),
        'peer_u': nrm((DEPTH, N_EXPERTS, D), D ** -0.5),
        'peer_v': nrm((DEPTH, N_EXPERTS, D), PEER_HEADS ** -0.5),
        'final_norm': 1.0 + nrm((D,), 0.02),
    }


def reference(x_prompt, x_sample, cache_ckv, cache_kpe, c, c_ctx, w_ada, b_ada, norm1, norm2, w_in,
              a_ws, a_bs, b_conv_w, b_conv_b, b_filt_w1, b_filt_b1, b_filt_w2, b_filt_b2, b_filt_w3, b_skip,
              q_norm, kv_norm, w_uq, w_uk, w_uv, w_branch_a, w_branch_b, w_branch_c, w_out,
              peer_wq, peer_sub_keys, peer_u, peer_v, final_norm):
    rope = axial_rope_tables(x_sample.shape[1], x_sample.dtype)
    ctx_cond = c_ctx[None, :]
    xp, xs = x_prompt, x_sample
    new_ckv, new_kpe = [], []
    for l in range(DEPTH):
        lp = {
            'w_ada': w_ada[l], 'b_ada': b_ada[l], 'norm1': norm1[l], 'norm2': norm2[l], 'w_in': w_in[l],
            'a_ws': a_ws[l], 'a_bs': a_bs[l], 'b_conv_w': b_conv_w[l], 'b_conv_b': b_conv_b[l],
            'b_filt_w1': b_filt_w1[l], 'b_filt_b1': b_filt_b1[l], 'b_filt_w2': b_filt_w2[l],
            'b_filt_b2': b_filt_b2[l], 'b_filt_w3': b_filt_w3[l], 'b_skip': b_skip[l],
            'q_norm': q_norm[l], 'kv_norm': kv_norm[l], 'w_uq': w_uq[l], 'w_uk': w_uk[l], 'w_uv': w_uv[l],
            'w_branch_a': w_branch_a[l], 'w_branch_b': w_branch_b[l], 'w_branch_c': w_branch_c[l],
            'w_out': w_out[l], 'peer_wq': peer_wq[l], 'peer_sub_keys': peer_sub_keys[l],
            'peer_u': peer_u[l], 'peer_v': peer_v[l],
        }
        xp, ckv_l, kpe_l = trunk_layer(xp, ctx_cond, lp, None, None)
        new_ckv.append(ckv_l)
        new_kpe.append(kpe_l)
        xs, _, _ = trunk_layer(xs, c, lp, (cache_ckv[:, l], cache_kpe[:, l]), rope)
    y_prompt = rms_norm(xp, final_norm)
    y_sample = rms_norm(xs, final_norm)
    state_ckv = jnp.stack(new_ckv, axis=1)
    state_kpe = jnp.stack(new_kpe, axis=1)
    return (y_prompt, y_sample, state_ckv, state_kpe)
```

```python
import functools
import math

import jax
import jax.numpy as jnp
from jax import lax
from jax.experimental import pallas as pl
from jax.experimental.pallas import tpu as pltpu

F32 = jnp.float32
BF16 = jnp.bfloat16

D = 1024
SEG = 256
SEGS_PER_SAMPLE = 4
DEPTH = 4
GRID_W = 64
CHUNK = 128
EPS = 1e-6
A_WIDTH = 256
A_GROUPS = 4
B_WIDTH = 256
FILTER_EMB = 33
FILTER_FAST_DECAY = 0.3
FILTER_SLOW_DECAY = 1.5
FILTER_TARGET = 1e-2
HEADS = 8
Q_RANK = 256
KV_RANK = 128
NOPE = 64
ROPE = 32
V_DIM = 64
ROPE_BASE = 10000.0
GATE_COLS = 3 * D
A_COLS = 2 * A_WIDTH
B_COLS = 3 * B_WIDTH
PEER_HEADS = 8
N_KEYS = 128
N_EXPERTS = N_KEYS * N_KEYS
TOPK = 16
ATTN_SCALE = (NOPE + ROPE) ** -0.5
GELU_C = math.sqrt(2.0 / math.pi)
NEG_INF = float("-inf")

VMEM_LIMIT_BYTES = 56 * 1024 * 1024

PEER_TT = 512
PEER_TE = 512
ROUTE_TR = 256


def _params(*sem):
    return pltpu.CompilerParams(dimension_semantics=sem, vmem_limit_bytes=VMEM_LIMIT_BYTES)


def _dot(a, b):
    return jnp.dot(a, b, preferred_element_type=F32)


def _dot_nt(a, b):
    return lax.dot_general(a, b, (((1,), (1,)), ((), ())), preferred_element_type=F32)


def _rms(x, g):
    return x * lax.rsqrt(jnp.mean(x * x, axis=-1, keepdims=True) + EPS) * g


def _gelu(x):
    return 0.5 * x * (1.0 + jnp.tanh(GELU_C * (x + 0.044715 * (x * x * x))))


def _full(shape):
    n = len(shape)
    return pl.BlockSpec(shape, lambda *_: (0,) * n)


def _mod_kernel(c_ref, w_ref, b_ref, o_ref):
    c = c_ref[...]
    o_ref[...] = _dot(c * jax.nn.sigmoid(c), w_ref[...]) + b_ref[...]


def _modulation(cond, w_ada, b_ada):
    rows = cond.shape[0]
    out = pl.pallas_call(
        _mod_kernel,
        grid=(DEPTH, 6),
        in_specs=[
            pl.BlockSpec((rows, D), lambda l, j: (0, 0)),
            pl.BlockSpec((None, D, D), lambda l, j: (l, 0, j)),
            pl.BlockSpec((None, 1, D), lambda l, j: (l, 0, j)),
        ],
        out_specs=pl.BlockSpec((None, rows, D), lambda l, j: (l, 0, j)),
        out_shape=jax.ShapeDtypeStruct((DEPTH, rows, 6 * D), F32),
        compiler_params=_params("arbitrary", "arbitrary"),
        name="modulation",
    )(cond, w_ada, b_ada.reshape(DEPTH, 1, 6 * D))
    return out.reshape(DEPTH, rows, 6, D)


def _filter_kernel(z_ref, w1_ref, b1_ref, w2_ref, b2_ref, w3_ref, decay_ref, fc_ref, fs_ref, hr_ref, hi_ref):
    hp = lax.Precision.HIGHEST
    h = jnp.sin(jnp.dot(z_ref[...], w1_ref[...], precision=hp, preferred_element_type=F32) + b1_ref[...])
    h = jnp.sin(jnp.dot(h, w2_ref[...], precision=hp, preferred_element_type=F32) + b2_ref[...])
    h = jnp.dot(h, w3_ref[...], precision=hp, preferred_element_type=F32)
    decay = decay_ref[...]
    fwd = h[:, :B_WIDTH] * decay
    bwd = h[:, B_WIDTH:] * decay
    row = lax.broadcasted_iota(jnp.int32, bwd.shape, 0)
    bwd = jnp.where(row == 0, 0.0, bwd)
    hr_ref[...] = jnp.dot(fc_ref[...], fwd + bwd, precision=hp, preferred_element_type=F32)
    hi_ref[...] = -jnp.dot(fs_ref[...], fwd - bwd, precision=hp, preferred_element_type=F32)


def _filter_spectrum(seq, tabs, w1p, b1, w2, b2, w3):
    lf = tabs["lf"]
    return pl.pallas_call(
        _filter_kernel,
        grid=(DEPTH,),
        in_specs=[
            _full((seq, 128)),
            pl.BlockSpec((None, 128, 64), lambda l: (l, 0, 0)),
            pl.BlockSpec((None, 1, 64), lambda l: (l, 0, 0)),
            pl.BlockSpec((None, 64, 64), lambda l: (l, 0, 0)),
            pl.BlockSpec((None, 1, 64), lambda l: (l, 0, 0)),
            pl.BlockSpec((None, 64, 2 * B_WIDTH), lambda l: (l, 0, 0)),
            _full((seq, B_WIDTH)),
            _full((lf, seq)),
            _full((lf, seq)),
        ],
        out_specs=[pl.BlockSpec((None, lf, B_WIDTH), lambda l: (l, 0, 0))] * 2,
        out_shape=[jax.ShapeDtypeStruct((DEPTH, lf, B_WIDTH), F32)] * 2,
        compiler_params=_params("arbitrary"),
        name=f"filter_spectrum_{seq}",
    )(tabs["zfeat"], w1p, b1, w2, b2, w3, tabs["decay"], tabs["fc32"], tabs["fs32"])


def _hyena_tables(seq):
    lf = seq + 128
    t = jnp.linspace(0.0, 1.0, seq, dtype=F32)[:, None]
    bands = (FILTER_EMB - 1) // 2
    f = jnp.linspace(1e-4, bands - 1, bands, dtype=F32)
    w = 2.0 * math.pi * jnp.arange(seq, dtype=F32)[:, None] / seq
    z = jnp.concatenate([t, jnp.cos(f * w), -jnp.sin(f * w)], axis=-1)
    zfeat = jnp.pad(z, ((0, 0), (0, 128 - FILTER_EMB)))
    deltas = jnp.abs(jnp.linspace(math.log(FILTER_TARGET) / FILTER_FAST_DECAY,
                                  math.log(FILTER_TARGET) / FILTER_SLOW_DECAY, B_WIDTH, dtype=F32))
    decay = jnp.exp(-t * deltas)
    fi = jnp.arange(lf, dtype=jnp.int32)[:, None]
    ti = jnp.arange(seq, dtype=jnp.int32)[None, :]
    ang = ((fi * ti) % (2 * seq)).astype(F32) * (math.pi / seq)
    valid = fi <= seq
    fc = jnp.where(valid, jnp.cos(ang), 0.0)
    fs = jnp.where(valid, jnp.sin(ang), 0.0)
    wf = jnp.where((fi == 0) | (fi == seq), 1.0, 2.0) / (2.0 * seq)
    gc = (fc * wf).T
    gs = (-fs * wf).T
    return dict(lf=lf, zfeat=zfeat, decay=decay, fc32=fc, fs32=fs,
                fc=fc.astype(BF16), fs=fs.astype(BF16), gc=gc.astype(BF16), gs=gs.astype(BF16))


def _inproj_kernel(x_ref, mod_ref, n1_ref, wg_ref, wa_ref, wb_ref, wq_ref, wkv_ref, wpe_ref, wpes_ref,
                   qn_ref, kvn_ref, wuqn_ref, wuqp_ref, wuqps_ref, wukbd_ref, rc_ref, rs_ref, aws_ref, abias_ref,
                   gates_ref, ya_ref, zb_ref, qlat_ref, qpe_ref, ckv_ref, kpe8_ref, kper_ref):
    mod = mod_ref[...]
    h = _rms(x_ref[...], n1_ref[...]) * (1.0 + mod[1:2]) + mod[0:1]
    hb = h.astype(BF16)
    for j in range(3):
        cols = slice(j * D, (j + 1) * D)
        gates_ref[:, cols] = jax.nn.sigmoid(_dot(hb, wg_ref[:, cols]))
    za = _gelu(_dot(hb, wa_ref[...]))
    u = za[:, :A_WIDTH]
    v = za[:, A_WIDTH:].astype(BF16)
    lane_group = lax.broadcasted_iota(jnp.int32, (CHUNK, A_WIDTH), 1) // (A_WIDTH // A_GROUPS)
    for n in range(SEG // CHUNK):
        rows = slice(n * CHUNK, (n + 1) * CHUNK)
        mixed = abias_ref[...]
        for g in range(A_GROUPS):
            mixed = mixed + jnp.where(lane_group == g, _dot(aws_ref[g], v[rows]), 0.0)
        ya_ref[rows, :] = u[rows] * mixed
    zb_ref[...] = _dot(hb, wb_ref[...])
    rc = rc_ref[...]
    rs = rs_ref[...]
    qn = _rms(_dot(hb, wq_ref[...]), qn_ref[...]).astype(BF16)
    q_nope = _dot(qn, wuqn_ref[...]).astype(BF16)
    qlat_ref[...] = _dot(q_nope, wukbd_ref[...]).astype(BF16)
    qpe_ref[...] = (_dot(qn, wuqp_ref[...]) * rc + _dot(qn, wuqps_ref[...]) * rs).astype(BF16)
    ckv_ref[...] = _rms(_dot(hb, wkv_ref[...]), kvn_ref[...])
    kp = _dot(hb, wpe_ref[...])
    kper_ref[...] = kp[:, :ROPE]
    kpe8_ref[...] = (kp * rc + _dot(hb, wpes_ref[...]) * rs).astype(BF16)


def _inproj(x, mod_l, wl, consts, nseg, bp):
    n = x.shape[0]
    cond_row = lambda s: jnp.where(s < bp, 0, 1 + (s - bp) // SEGS_PER_SAMPLE)
    rope_blk = lambda s: jnp.where(s < bp, 0, 1 + (s - bp) % SEGS_PER_SAMPLE)
    row_blk = lambda w: pl.BlockSpec((SEG, w), lambda s: (s, 0))
    in_specs = [
        row_blk(D),
        pl.BlockSpec((None, 6, D), lambda s: (cond_row(s), 0, 0)),
        _full((1, D)),
        _full((D, GATE_COLS)), _full((D, A_COLS)), _full((D, B_COLS)), _full((D, Q_RANK)), _full((D, KV_RANK)),
        _full((D, HEADS * ROPE)), _full((D, HEADS * ROPE)),
        _full((1, Q_RANK)), _full((1, KV_RANK)),
        _full((Q_RANK, HEADS * NOPE)), _full((Q_RANK, HEADS * ROPE)), _full((Q_RANK, HEADS * ROPE)),
        _full((HEADS * NOPE, HEADS * KV_RANK)),
        pl.BlockSpec((SEG, HEADS * ROPE), lambda s: (rope_blk(s), 0)),
        pl.BlockSpec((SEG, HEADS * ROPE), lambda s: (rope_blk(s), 0)),
        _full((A_GROUPS, CHUNK, CHUNK)), _full((CHUNK, A_WIDTH)),
    ]
    outs = [(GATE_COLS, F32), (A_WIDTH, F32), (B_COLS, F32), (HEADS * KV_RANK, BF16), (HEADS * ROPE, BF16),
            (KV_RANK, F32), (HEADS * ROPE, BF16), (ROPE, F32)]
    return pl.pallas_call(
        _inproj_kernel,
        grid=(nseg,),
        in_specs=in_specs,
        out_specs=[row_blk(w) for w, _ in outs],
        out_shape=[jax.ShapeDtypeStruct((n, w), dt) for w, dt in outs],
        compiler_params=_params("arbitrary"),
        name="inproj",
    )(x, mod_l, wl["norm1"], wl["wg"], wl["wa"], wl["wb"], wl["wq"], wl["wkv"], wl["wpe8"], wl["wpe8s"],
      wl["q_norm"], wl["kv_norm"], wl["wuq_n"], wl["wuq_pe"], wl["wuq_pes"], wl["wuk_bd"],
      consts["rope_c"], consts["rope_s"], wl["a_ws"], wl["a_bias"])


def _hyena_kernel(zb_ref, cw_ref, cb_ref, skip_ref, hr_ref, hi_ref, fc_ref, fs_ref, gc_ref, gs_ref, yb_ref, *, seq):
    zb = zb_ref[...]
    row = lax.broadcasted_iota(jnp.int32, zb.shape, 0)
    prev = jnp.where(row == 0, 0.0, pltpu.roll(zb, 1, 0))
    nxt = jnp.where(row == seq - 1, 0.0, pltpu.roll(zb, seq - 1, 0))
    cw = cw_ref[...]
    z = cb_ref[...] + prev * cw[0:1] + zb * cw[1:2] + nxt * cw[2:3]
    x0 = z[:, :B_WIDTH]
    u = z[:, B_WIDTH:2 * B_WIDTH] * z[:, 2 * B_WIDTH:]
    ub = u.astype(BF16)
    ur = _dot(fc_ref[...], ub)
    ui = -_dot(fs_ref[...], ub)
    hr = hr_ref[...]
    hi = hi_ref[...]
    yr = (ur * hr - ui * hi).astype(BF16)
    yi = (ur * hi + ui * hr).astype(BF16)
    y = _dot(gc_ref[...], yr) + _dot(gs_ref[...], yi)
    yb_ref[...] = x0 * (y + u * skip_ref[...])


def _hyena(zb, wl, tabs, hr, hi, seq, nb, blk0):
    lf = tabs["lf"]
    return pl.pallas_call(
        functools.partial(_hyena_kernel, seq=seq),
        grid=(nb,),
        in_specs=[
            pl.BlockSpec((seq, B_COLS), lambda b: (blk0 + b, 0)),
            _full((3, B_COLS)), _full((1, B_COLS)), _full((1, B_WIDTH)),
            _full((lf, B_WIDTH)), _full((lf, B_WIDTH)),
            _full((lf, seq)), _full((lf, seq)), _full((seq, lf)), _full((seq, lf)),
        ],
        out_specs=pl.BlockSpec((seq, B_WIDTH), lambda b: (b, 0)),
        out_shape=jax.ShapeDtypeStruct((nb * seq, B_WIDTH), F32),
        compiler_params=_params("arbitrary"),
        name=f"hyena_{seq}",
    )(zb, wl["b_conv_w"], wl["b_conv_b"], wl["b_skip"], hr, hi, tabs["fc"], tabs["fs"], tabs["gc"], tabs["gs"])


def _attn_kernel(*refs, has_ctx):
    if has_ctx:
        qlat_ref, qpe_ref, ckv_ref, kpe8_ref, cckv_ref, ckpe8_ref, wuv_ref, yc_ref = refs
    else:
        qlat_ref, qpe_ref, ckv_ref, kpe8_ref, wuv_ref, yc_ref = refs
    kb = ckv_ref[...].astype(BF16)
    kp = kpe8_ref[...]
    if has_ctx:
        ckb = cckv_ref[...]
        ckp = ckpe8_ref[...]
    qp_all = qpe_ref[...]
    lane_head = lax.broadcasted_iota(jnp.int32, qp_all.shape, 1) // ROPE
    heads = []
    for h in range(HEADS):
        ql = qlat_ref[:, h * KV_RANK:(h + 1) * KV_RANK]
        qp = jnp.where(lane_head == h, qp_all, jnp.zeros_like(qp_all))
        s = (_dot_nt(ql, kb) + _dot_nt(qp, kp)) * ATTN_SCALE
        m = jnp.max(s, axis=-1, keepdims=True)
        if has_ctx:
            s2 = (_dot_nt(ql, ckb) + _dot_nt(qp, ckp)) * ATTN_SCALE
            m = jnp.maximum(m, jnp.max(s2, axis=-1, keepdims=True))
        e = jnp.exp(s - m)
        den = jnp.sum(e, axis=-1, keepdims=True)
        o = _dot(e.astype(BF16), kb)
        if has_ctx:
            e2 = jnp.exp(s2 - m)
            den = den + jnp.sum(e2, axis=-1, keepdims=True)
            o = o + _dot(e2.astype(BF16), ckb)
        heads.append((o / den).astype(BF16))
    yc_ref[...] = _dot(jnp.concatenate(heads, axis=1), wuv_ref[...])


def _attention(qlat, qpe, ckv, kpe8, wuv_bd, seq, nb, row0, ctx=None):
    tq = SEG
    nq = seq // tq
    q0 = row0 // tq
    k0 = row0 // seq
    in_specs = [
        pl.BlockSpec((tq, HEADS * KV_RANK), lambda b, i: (q0 + b * nq + i, 0)),
        pl.BlockSpec((tq, HEADS * ROPE), lambda b, i: (q0 + b * nq + i, 0)),
        pl.BlockSpec((seq, KV_RANK), lambda b, i: (k0 + b, 0)),
        pl.BlockSpec((seq, HEADS * ROPE), lambda b, i: (k0 + b, 0)),
    ]
    args = [qlat, qpe, ckv, kpe8]
    if ctx is not None:
        past = ctx[0].shape[1]
        in_specs += [pl.BlockSpec((None, past, KV_RANK), lambda b, i: (b, 0, 0)),
                     pl.BlockSpec((None, past, HEADS * ROPE), lambda b, i: (b, 0, 0))]
        args += list(ctx)
    in_specs.append(_full((HEADS * KV_RANK, HEADS * V_DIM)))
    args.append(wuv_bd)
    return pl.pallas_call(
        functools.partial(_attn_kernel, has_ctx=ctx is not None),
        grid=(nb, nq),
        in_specs=in_specs,
        out_specs=pl.BlockSpec((tq, HEADS * V_DIM), lambda b, i: (b * nq + i, 0)),
        out_shape=jax.ShapeDtypeStruct((nb * seq, HEADS * V_DIM), F32),
        compiler_params=_params("arbitrary", "arbitrary"),
        name=f"attention_{seq}",
    )(*args)


def _merge_kernel(x_ref, mod_ref, gates_ref, ya_ref, yb_ref, yc_ref, wa_ref, wb_ref, wc_ref, wo_ref, n2_ref,
                  wpq_ref, subk_ref, x1_ref, h2t_ref, st_ref):
    mod = mod_ref[...]
    merged = (gates_ref[:, 0:D] * _dot(ya_ref[...].astype(BF16), wa_ref[...])
              + gates_ref[:, D:2 * D] * _dot(yb_ref[...].astype(BF16), wb_ref[...])
              + gates_ref[:, 2 * D:3 * D] * _dot(yc_ref[...].astype(BF16), wc_ref[...]))
    x1 = x_ref[...] + mod[2:3] * _dot(merged.astype(BF16), wo_ref[...])
    x1_ref[...] = x1
    h2 = _rms(x1, n2_ref[...]) * (1.0 + mod[4:5]) + mod[3:4]
    h2t_ref[...] = h2.T.astype(BF16)
    q = _dot(h2.astype(BF16), wpq_ref[...]).astype(BF16)
    for g in range(2 * PEER_HEADS):
        rows = slice(g * N_KEYS, (g + 1) * N_KEYS)
        st_ref[rows, :] = _dot_nt(subk_ref[g % 2], q[:, rows])


def _merge(x, mod_l, gates, ya, yb, yc, wl, nseg, bp):
    n = x.shape[0]
    cond_row = lambda s: jnp.where(s < bp, 0, 1 + (s - bp) // SEGS_PER_SAMPLE)
    row_blk = lambda w: pl.BlockSpec((SEG, w), lambda s: (s, 0))
    col_blk = lambda r: pl.BlockSpec((r, SEG), lambda s: (0, s))
    nq = 2 * PEER_HEADS * N_KEYS
    return pl.pallas_call(
        _merge_kernel,
        grid=(nseg,),
        in_specs=[
            row_blk(D),
            pl.BlockSpec((None, 6, D), lambda s: (cond_row(s), 0, 0)),
            row_blk(GATE_COLS), row_blk(A_WIDTH), row_blk(B_WIDTH), row_blk(HEADS * V_DIM),
            _full((A_WIDTH, D)), _full((B_WIDTH, D)), _full((HEADS * V_DIM, D)), _full((D, D)), _full((1, D)),
            _full((D, nq)), _full((2, N_KEYS, N_KEYS)),
        ],
        out_specs=[row_blk(D), col_blk(D), col_blk(nq)],
        out_shape=[jax.ShapeDtypeStruct((n, D), F32), jax.ShapeDtypeStruct((D, n), BF16),
                   jax.ShapeDtypeStruct((nq, n), F32)],
        compiler_params=_params("arbitrary"),
        name="merge",
    )(x, mod_l, gates, ya, yb, yc, wl["w_branch_a"], wl["w_branch_b"], wl["w_branch_c"], wl["w_out"],
      wl["norm2"], wl["peer_wq"], wl["peer_sub_keys"])


def _top16(x, rank_init=None):
    vals = []
    rank = rank_init
    for k in range(TOPK):
        m = jnp.max(x, axis=0, keepdims=True)
        hit = x == m
        if rank is not None:
            rank = jnp.where(hit, float(k), rank)
        x = jnp.where(hit, NEG_INF, x)
        vals.append(m)
    return vals, rank


def _route_kernel(st_ref, r_ref, d_ref, k2_ref, e2_ref):
    for h in range(PEER_HEADS):
        s1 = st_ref[(2 * h) * N_KEYS:(2 * h + 1) * N_KEYS, :]
        s2 = st_ref[(2 * h + 1) * N_KEYS:(2 * h + 2) * N_KEYS, :]
        v1, _ = _top16(s1)
        v2, rank2 = _top16(s2, jnp.full(s2.shape, float(TOPK), F32))
        v1a = jnp.concatenate(v1, axis=0)
        v2a = jnp.concatenate(v2, axis=0)
        cand = jnp.concatenate([v1a + v2[0]] + [v1a[0:8] + v2[b] for b in range(1, 8)] + [v1[0] + v2a[8:16]],
                               axis=0)
        top = cand
        for k in range(TOPK):
            thr = jnp.max(top, axis=0, keepdims=True)
            top = jnp.where(top == thr, NEG_INF, top)
        m0 = v1[0] + v2[0]
        z = jnp.sum(jnp.where(cand >= thr, jnp.exp(cand - m0), 0.0), axis=0, keepdims=True)
        r = jnp.zeros(s1.shape, F32)
        for b in range(TOPK):
            r = r + jnp.where(s1 + v2[b] >= thr, 1.0, 0.0)
        r_ref[h] = r
        d_ref[h] = jnp.exp(s1 - v1[0]) / z
        k2_ref[h] = rank2.astype(BF16)
        e2_ref[h] = jnp.exp(s2 - v2[0]).astype(BF16)


def _route(st):
    n = st.shape[1]
    blk = pl.BlockSpec((PEER_HEADS, N_KEYS, ROUTE_TR), lambda t: (0, 0, t))
    shape = (PEER_HEADS, N_KEYS, n)
    return pl.pallas_call(
        _route_kernel,
        grid=(n // ROUTE_TR,),
        in_specs=[pl.BlockSpec((2 * PEER_HEADS * N_KEYS, ROUTE_TR), lambda t: (0, t))],
        out_specs=[blk] * 4,
        out_shape=[jax.ShapeDtypeStruct(shape, F32), jax.ShapeDtypeStruct(shape, F32),
                   jax.ShapeDtypeStruct(shape, BF16), jax.ShapeDtypeStruct(shape, BF16)],
        compiler_params=_params("arbitrary"),
        name="route",
    )(st)


def _peer_kernel(h2t_ref, u_ref, vt_ref, r_ref, d_ref, k2_ref, e2_ref, x1_ref, mod_ref, x2_ref, acc_ref, p_ref):
    e = pl.program_id(1)

    @pl.when(e == 0)
    def _():
        acc_ref[...] = jnp.zeros_like(acc_ref)

    at = _dot(u_ref[...], h2t_ref[...])
    rows_per_step = PEER_TE // N_KEYS
    for ii in range(rows_per_step):
        i = e * rows_per_step + ii
        w = jnp.zeros((N_KEYS, PEER_TT), BF16)
        for h in range(PEER_HEADS):
            rb = r_ref[h, pl.ds(i, 1), :].astype(BF16)
            db = d_ref[h, pl.ds(i, 1), :].astype(BF16)
            w = w + jnp.where(k2_ref[h] < rb, e2_ref[h], jnp.zeros((), BF16)) * db
        rows = slice(ii * N_KEYS, (ii + 1) * N_KEYS)
        p_ref[rows, :] = w * _gelu(at[rows]).astype(BF16)
    acc_ref[...] += _dot(vt_ref[...], p_ref[...])

    @pl.when(e == pl.num_programs(1) - 1)
    def _():
        x2_ref[...] = x1_ref[...] + mod_ref[...][5:6] * acc_ref[...].T


def _peer(h2t, u_b, vt_b, r, d, k2, e2, x1, mod_l, bp):
    n = x1.shape[0]
    segs_per_tile = PEER_TT // SEG
    cond_row = lambda t: jnp.where(t * segs_per_tile < bp, 0, 1 + (t * segs_per_tile - bp) // SEGS_PER_SAMPLE)
    rt = pl.BlockSpec((PEER_HEADS, N_KEYS, PEER_TT), lambda t, e: (0, 0, t))
    return pl.pallas_call(
        _peer_kernel,
        grid=(n // PEER_TT, N_EXPERTS // PEER_TE),
        in_specs=[
            pl.BlockSpec((D, PEER_TT), lambda t, e: (0, t)),
            pl.BlockSpec((PEER_TE, D), lambda t, e: (e, 0)),
            pl.BlockSpec((D, PEER_TE), lambda t, e: (0, e)),
            rt, rt, rt, rt,
            pl.BlockSpec((PEER_TT, D), lambda t, e: (t, 0)),
            pl.BlockSpec((None, 6, D), lambda t, e: (cond_row(t), 0, 0)),
        ],
        out_specs=pl.BlockSpec((PEER_TT, D), lambda t, e: (t, 0)),
        out_shape=jax.ShapeDtypeStruct((n, D), F32),
        scratch_shapes=[pltpu.VMEM((D, PEER_TT), F32), pltpu.VMEM((PEER_TE, PEER_TT), BF16)],
        compiler_params=_params("arbitrary", "arbitrary"),
        name="peer",
    )(h2t, u_b, vt_b, r, d, k2, e2, x1, mod_l)


def _final_norm_kernel(x_ref, g_ref, y_ref):
    y_ref[...] = _rms(x_ref[...], g_ref[...])


def _final_norm(x, g):
    n = x.shape[0]
    return pl.pallas_call(
        _final_norm_kernel,
        grid=(n // SEG,),
        in_specs=[pl.BlockSpec((SEG, D), lambda s: (s, 0)), _full((1, D))],
        out_specs=pl.BlockSpec((SEG, D), lambda s: (s, 0)),
        out_shape=jax.ShapeDtypeStruct((n, D), F32),
        compiler_params=_params("arbitrary"),
        name="final_norm",
    )(x, g.reshape(1, D))


def _rope_tables(seq):
    rows = seq // GRID_W
    row = jnp.broadcast_to(jnp.arange(rows)[:, None], (rows, GRID_W)).reshape(-1)
    col = jnp.broadcast_to(jnp.arange(GRID_W)[None, :], (rows, GRID_W)).reshape(-1)
    quarter = ROPE // 4
    inv_freq = ROPE_BASE ** (-jnp.arange(quarter, dtype=F32) / quarter)
    ang = jnp.stack([row, col], axis=-1).astype(F32)[:, :, None] * inv_freq
    cos, sin = jnp.cos(ang), jnp.sin(ang)
    c32 = jnp.stack([cos, cos], axis=2).reshape(seq, ROPE)
    s32 = jnp.stack([-sin, sin], axis=2).reshape(seq, ROPE)
    c = jnp.concatenate([jnp.ones((SEG, ROPE), F32), c32], axis=0)
    s = jnp.concatenate([jnp.zeros((SEG, ROPE), F32), s32], axis=0)
    return jnp.tile(c, (1, HEADS)), jnp.tile(s, (1, HEADS))


def _half_swap_perm():
    d = jnp.arange(ROPE)
    return jnp.where((d % (ROPE // 2)) < ROPE // 4, d + ROPE // 4, d - ROPE // 4)


def _layer_weights(l, p):
    w_in = p["w_in"][l]
    c0 = GATE_COLS
    c1 = c0 + A_COLS
    c2 = c1 + B_COLS
    c3 = c2 + Q_RANK
    c4 = c3 + KV_RANK
    perm = _half_swap_perm()
    wpe = w_in[:, c4:]
    w_uq = p["w_uq"][l].reshape(Q_RANK, HEADS, NOPE + ROPE)
    wuq_pe = w_uq[:, :, NOPE:]
    eye = jnp.eye(HEADS, dtype=F32)
    wuk_bd = jnp.einsum("hnc,hg->hngc", jnp.transpose(p["w_uk"][l], (1, 2, 0)), eye)
    wuv_bd = jnp.einsum("hcv,hg->hcgv", jnp.transpose(p["w_uv"][l], (1, 0, 2)), eye)
    return dict(
        norm1=p["norm1"][l].reshape(1, D), norm2=p["norm2"][l].reshape(1, D),
        wg=w_in[:, :c0].astype(BF16), wa=w_in[:, c0:c1].astype(BF16), wb=w_in[:, c1:c2].astype(BF16),
        wq=w_in[:, c2:c3].astype(BF16), wkv=w_in[:, c3:c4].astype(BF16),
        wpe8=jnp.tile(wpe, (1, HEADS)).astype(BF16), wpe8s=jnp.tile(wpe[:, perm], (1, HEADS)).astype(BF16),
        q_norm=p["q_norm"][l].reshape(1, Q_RANK), kv_norm=p["kv_norm"][l].reshape(1, KV_RANK),
        wuq_n=w_uq[:, :, :NOPE].reshape(Q_RANK, HEADS * NOPE).astype(BF16),
        wuq_pe=wuq_pe.reshape(Q_RANK, HEADS * ROPE).astype(BF16),
        wuq_pes=wuq_pe[:, :, perm].reshape(Q_RANK, HEADS * ROPE).astype(BF16),
        wuk_bd=wuk_bd.reshape(HEADS * NOPE, HEADS * KV_RANK).astype(BF16),
        wuv_bd=wuv_bd.reshape(HEADS * KV_RANK, HEADS * V_DIM).astype(BF16),
        a_ws=p["a_ws"][l].astype(BF16),
        a_bias=jnp.repeat(p["a_bs"][l].T, A_WIDTH // A_GROUPS, axis=1),
        b_conv_w=p["b_conv_w"][l], b_conv_b=p["b_conv_b"][l].reshape(1, B_COLS),
        b_skip=p["b_skip"][l].reshape(1, B_WIDTH),
        w_branch_a=p["w_branch_a"][l].astype(BF16), w_branch_b=p["w_branch_b"][l].astype(BF16),
        w_branch_c=p["w_branch_c"][l].astype(BF16), w_out=p["w_out"][l].astype(BF16),
        peer_wq=p["peer_wq"][l].astype(BF16), peer_sub_keys=p["peer_sub_keys"][l].astype(BF16),
        peer_u=p["peer_u"][l].astype(BF16), peer_vt=p["peer_v"][l].T.astype(BF16),
    )


@jax.jit
def _forward(x_prompt, x_sample, cache_ckv, cache_kpe, c, c_ctx, p):
    bp, seq_p, _ = x_prompt.shape
    bs, seq_s, _ = x_sample.shape
    assert seq_p == SEG and seq_s == SEG * SEGS_PER_SAMPLE and bp % SEGS_PER_SAMPLE == 0
    n_p = bp * seq_p
    nseg = bp + bs * SEGS_PER_SAMPLE
    assert (nseg * SEG) % PEER_TT == 0 and bp % (PEER_TT // SEG) == 0

    x = jnp.concatenate([x_prompt.reshape(n_p, D), x_sample.reshape(bs * seq_s, D)], axis=0)
    cond = jnp.concatenate([c_ctx[None, :], c], axis=0)
    cond = jnp.pad(cond, ((0, (-cond.shape[0]) % 8), (0, 0)))
    mod = _modulation(cond, p["w_ada"], p["b_ada"])

    rope_c, rope_s = _rope_tables(seq_s)
    consts = dict(rope_c=rope_c, rope_s=rope_s)
    tabs_p, tabs_s = _hyena_tables(seq_p), _hyena_tables(seq_s)
    w1p = jnp.pad(p["b_filt_w1"], ((0, 0), (0, 128 - FILTER_EMB), (0, 0)))
    filt_args = (w1p, p["b_filt_b1"][:, None, :], p["b_filt_w2"], p["b_filt_b2"][:, None, :], p["b_filt_w3"])
    hr_p, hi_p = _filter_spectrum(seq_p, tabs_p, *filt_args)
    hr_s, hi_s = _filter_spectrum(seq_s, tabs_s, *filt_args)

    state_ckv, state_kpe = [], []
    for l in range(DEPTH):
        wl = _layer_weights(l, p)
        gates, ya, zb, qlat, qpe, ckv, kpe8, kpe_raw = _inproj(x, mod[l], wl, consts, nseg, bp)
        state_ckv.append(ckv[:n_p].reshape(bp, seq_p, KV_RANK))
        state_kpe.append(kpe_raw[:n_p].reshape(bp, seq_p, ROPE))
        yb = jnp.concatenate([
            _hyena(zb, wl, tabs_p, hr_p[l], hi_p[l], seq_p, bp, 0),
            _hyena(zb, wl, tabs_s, hr_s[l], hi_s[l], seq_s, bs, n_p // seq_s)], axis=0)
        ctx = (cache_ckv[:, l].astype(BF16), jnp.tile(cache_kpe[:, l], (1, 1, HEADS)).astype(BF16))
        yc = jnp.concatenate([
            _attention(qlat, qpe, ckv, kpe8, wl["wuv_bd"], seq_p, bp, 0),
            _attention(qlat, qpe, ckv, kpe8, wl["wuv_bd"], seq_s, bs, n_p, ctx)], axis=0)
        x1, h2t, st = _merge(x, mod[l], gates, ya, yb, yc, wl, nseg, bp)
        r, d, k2, e2 = _route(st)
        x = _peer(h2t, wl["peer_u"], wl["peer_vt"], r, d, k2, e2, x1, mod[l], bp)

    y = _final_norm(x, p["final_norm"])
    return (y[:n_p].reshape(bp, seq_p, D), y[n_p:].reshape(bs, seq_s, D),
            jnp.stack(state_ckv, axis=1), jnp.stack(state_kpe, axis=1))


def kernel(x_prompt, x_sample, cache_ckv, cache_kpe, c, c_ctx, w_ada, b_ada, norm1, norm2, w_in, a_ws, a_bs, b_conv_w, b_conv_b, b_filt_w1, b_filt_b1, b_filt_w2, b_filt_b2, b_filt_w3, b_skip, q_norm, kv_norm, w_uq, w_uk, w_uv, w_branch_a, w_branch_b, w_branch_c, w_out, peer_wq, peer_sub_keys, peer_u, peer_v, final_norm):
    p = dict(w_ada=w_ada, b_ada=b_ada, norm1=norm1, norm2=norm2, w_in=w_in, a_ws=a_ws, a_bs=a_bs,
             b_conv_w=b_conv_w, b_conv_b=b_conv_b, b_filt_w1=b_filt_w1, b_filt_b1=b_filt_b1,
             b_filt_w2=b_filt_w2, b_filt_b2=b_filt_b2, b_filt_w3=b_filt_w3, b_skip=b_skip,
             q_norm=q_norm, kv_norm=kv_norm, w_uq=w_uq, w_uk=w_uk, w_uv=w_uv,
             w_branch_a=w_branch_a, w_branch_b=w_branch_b, w_branch_c=w_branch_c, w_out=w_out,
             peer_wq=peer_wq, peer_sub_keys=peer_sub_keys, peer_u=peer_u, peer_v=peer_v,
             final_norm=final_norm)
    return _forward(x_prompt, x_sample, cache_ckv, cache_kpe, c, c_ctx, p)
```
